```python
import math
import jax
import jax.numpy as jnp
from jax import lax
import numpy as np

D_MODEL = 2048
BATCH = 8
SEQ = 2048
DEPTH = 4

GRID_W = 64
CTX_LEN = 256
N_EVEN = (DEPTH + 1) // 2
N_ODD = DEPTH // 2

Q_BLOCK = 128
ROPE_THETA = 10000.0
NORM_EPS = 1e-6
NEG_INF = -1e30

A_HEADS = 8
A_KV_HEADS = 2
A_HEAD_DIM = 128
A_Q_W = A_HEADS * A_HEAD_DIM
A_KV_W = A_KV_HEADS * A_HEAD_DIM
S5_W = 1024
S5_GROUP = 16
S5_GROUPS = S5_W // S5_GROUP
S5_STATE = 64
C_HEADS = 16
C_KV_HEADS = 2
C_HEAD_DIM = 64
C_Q_W = C_HEADS * C_HEAD_DIM
C_KV_W = C_KV_HEADS * C_HEAD_DIM
C_WINDOW = 128
M_INNER = 1024
M_HEAD_DIM = 64
M_HEADS = M_INNER // M_HEAD_DIM
M_GROUPS = 2
M_STATE = 128
M_CONV = 3
M_CHUNK = 128
M_XBC = M_INNER + 2 * M_GROUPS * M_STATE

EVEN_SPLITS = (A_Q_W, A_KV_W, A_KV_W, S5_W)
ODD_SPLITS = (C_Q_W, C_KV_W, C_KV_W, M_INNER, M_XBC, 2 * M_HEADS)
EVEN_IN = sum(EVEN_SPLITS)
ODD_IN = sum(ODD_SPLITS)
EVEN_MIX = A_Q_W + S5_W
ODD_MIX = C_Q_W + M_INNER

N_EXPERTS = 16
EXPERT_FF = 1024
CAPACITY_FACTOR = 2

kernel_name = "hybrid_diffusion_trunk"


def rms_norm(x, g):
    xf = x.astype(jnp.float32)
    y = xf * lax.rsqrt(jnp.mean(xf * xf, axis=-1, keepdims=True) + NORM_EPS)
    return (y * g.astype(jnp.float32)).astype(x.dtype)


def split_cols(p, widths):
    return jnp.split(p, np.cumsum(widths)[:-1].tolist(), axis=-1)


def axial_rope_tables(n_tokens, head_dim):
    rows = n_tokens // GRID_W
    n_freq = head_dim // 4
    inv = ROPE_THETA ** (-jnp.arange(n_freq, dtype=jnp.float32) / n_freq)
    row = jnp.repeat(jnp.arange(rows, dtype=jnp.float32), GRID_W)
    col = jnp.tile(jnp.arange(GRID_W, dtype=jnp.float32), rows)
    ang = jnp.concatenate([row[:, None] * inv, col[:, None] * inv], axis=-1)
    return jnp.cos(ang), jnp.sin(ang)


def apply_rope(x, cos, sin):
    half = x.shape[-1] // 2
    x1, x2 = x[..., :half], x[..., half:]
    cos = cos[None, :, None, :].astype(x.dtype)
    sin = sin[None, :, None, :].astype(x.dtype)
    return jnp.concatenate([x1 * cos - x2 * sin, x1 * sin + x2 * cos], axis=-1)


def gqa_logits(q, k):
    scale = q.shape[-1] ** -0.5
    return jnp.einsum('bqgrd,bkgd->bgrqk', q, k, preferred_element_type=jnp.float32) * scale


def gqa_values(p, v):
    return jnp.einsum('bgrqk,bkgd->bqgrd', p.astype(v.dtype), v)


def mixer_a(q_c, k_c, v_c, q_x, k_x, v_x, q_norm_g, k_norm_g, with_ctx_out):
    bsz, n_lat = q_x.shape[:2]
    n_ctx = q_c.shape[1]
    rep = A_HEADS // A_KV_HEADS

    def heads(q, k, v):
        n = q.shape[1]
        q = rms_norm(q.reshape(bsz, n, A_HEADS, A_HEAD_DIM), q_norm_g)
        k = rms_norm(k.reshape(bsz, n, A_KV_HEADS, A_HEAD_DIM), k_norm_g)
        return q, k, v.reshape(bsz, n, A_KV_HEADS, A_HEAD_DIM)

    qc, kc, vc = heads(q_c, k_c, v_c)
    qx, kx, vx = heads(q_x, k_x, v_x)
    cos, sin = axial_rope_tables(n_lat, A_HEAD_DIM)
    qx = apply_rope(qx, cos, sin)
    kx = apply_rope(kx, cos, sin)
    k_all = jnp.concatenate([kc, kx], axis=1)
    v_all = jnp.concatenate([vc, vx], axis=1)
    nb = n_lat // Q_BLOCK
    qb = qx.reshape(bsz, nb, Q_BLOCK, A_KV_HEADS, rep, A_HEAD_DIM).swapaxes(0, 1)

    def block(q):
        return gqa_values(jax.nn.softmax(gqa_logits(q, k_all), axis=-1), v_all)

    out_x = lax.map(block, qb).swapaxes(0, 1).reshape(bsz, n_lat, A_Q_W)
    out_c = None
    if with_ctx_out:
        qcg = qc.reshape(bsz, n_ctx, A_KV_HEADS, rep, A_HEAD_DIM)
        out_c = gqa_values(jax.nn.softmax(gqa_logits(qcg, kc), axis=-1), vc).reshape(bsz, n_ctx, A_Q_W)
    return out_c, out_x


def _linear_combine(e1, e2):
    a1, b1 = e1
    a2, b2 = e2
    return a1 * a2, a2 * b1 + b2


def s5_discretise(a_re, a_im, log_dt, b_re, b_im):
    lam = lax.complex(a_re.astype(jnp.float32), a_im.astype(jnp.float32))
    dt = jnp.exp(log_dt.astype(jnp.float32))[:, None]
    lam_bar = jnp.exp(lam * dt)
    b = lax.complex(b_re.astype(jnp.float32), b_im.astype(jnp.float32))
    b_bar = ((lam_bar - 1.0) / lam)[..., None] * b
    return lam_bar, b_bar


def s5_scan(u, lam_bar, b_bar, h0, reverse):
    bu = jnp.einsum('gph,blgh->blgp', b_bar, u.astype(jnp.complex64))
    if h0 is not None:
        edge = -1 if reverse else 0
        bu = bu.at[:, edge].add(lam_bar * h0)
    a = jnp.broadcast_to(lam_bar, bu.shape)
    _, h = lax.associative_scan(_linear_combine, (a, bu), reverse=reverse, axis=1)
    return h


def mixer_b(u_c, u_x, a_re, a_im, log_dt, b_re, b_im, c_re, c_im, d_skip, glu_w, glu_b, with_ctx_out):
    bsz, n_lat = u_x.shape[:2]
    n_ctx = u_c.shape[1]
    uc = u_c.astype(jnp.float32).reshape(bsz, n_ctx, S5_GROUPS, S5_GROUP)
    ux = u_x.astype(jnp.float32).reshape(bsz, n_lat, S5_GROUPS, S5_GROUP)
    ys_x, ys_c = [], []
    for direction in range(2):
        reverse = direction == 1
        lam_bar, b_bar = s5_discretise(a_re[direction], a_im[direction], log_dt[direction],
                                       b_re[direction], b_im[direction])
        cm = lax.complex(c_re[direction].astype(jnp.float32), c_im[direction].astype(jnp.float32))
        h_c = s5_scan(uc, lam_bar, b_bar, None, reverse)
        h_end = h_c[:, 0] if reverse else h_c[:, -1]
        h_x = s5_scan(ux, lam_bar, b_bar, h_end, reverse)
        ys_x.append(jnp.einsum('ghp,blgp->blgh', cm, h_x).real)
        if with_ctx_out:
            ys_c.append(jnp.einsum('ghp,blgp->blgh', cm, h_c).real)

    def readout(y, u_in):
        y = y.reshape(u_in.shape).astype(u_in.dtype) + d_skip * u_in
        g = jax.nn.gelu(y)
        return g * jax.nn.sigmoid(g @ glu_w + glu_b)

    out_c = readout(ys_c[0] + ys_c[1], u_c) if with_ctx_out else None
    return out_c, readout(ys_x[0] + ys_x[1], u_x)


def mixer_c(q_c, k_c, v_c, q_x, k_x, v_x, sink, with_ctx_out):
    bsz, n_lat = q_x.shape[:2]
    n_ctx = q_c.shape[1]
    rep = C_HEADS // C_KV_HEADS
    kc = k_c.reshape(bsz, n_ctx, C_KV_HEADS, C_HEAD_DIM)
    vc = v_c.reshape(bsz, n_ctx, C_KV_HEADS, C_HEAD_DIM)
    cos, sin = axial_rope_tables(n_lat, C_HEAD_DIM)
    qx = apply_rope(q_x.reshape(bsz, n_lat, C_HEADS, C_HEAD_DIM), cos, sin)
    qx = qx.reshape(bsz, n_lat, C_KV_HEADS, rep, C_HEAD_DIM)
    kx = apply_rope(k_x.reshape(bsz, n_lat, C_KV_HEADS, C_HEAD_DIM), cos, sin)
    vx = v_x.reshape(bsz, n_lat, C_KV_HEADS, C_HEAD_DIM)
    sink_g = sink.astype(jnp.float32).reshape(C_KV_HEADS, rep)[None, :, :, None, None]

    def sink_softmax(s):
        sk = jnp.broadcast_to(sink_g, s.shape[:-1] + (1,))
        return jax.nn.softmax(jnp.concatenate([s, sk], axis=-1), axis=-1)[..., :-1]

    span = Q_BLOCK + 2 * C_WINDOW
    pad = ((0, 0), (C_WINDOW, C_WINDOW), (0, 0), (0, 0))
    kp = jnp.pad(kx, pad)
    vp = jnp.pad(vx, pad)
    nb = n_lat // Q_BLOCK
    qb = qx.reshape(bsz, nb, Q_BLOCK, C_KV_HEADS, rep, C_HEAD_DIM).swapaxes(0, 1)
    starts = jnp.arange(nb, dtype=jnp.int32) * Q_BLOCK
    q_off = jnp.arange(Q_BLOCK, dtype=jnp.int32)
    k_off = jnp.arange(span, dtype=jnp.int32) - C_WINDOW

    def block(args):
        start, q = args
        kw = lax.dynamic_slice_in_dim(kp, start, span, axis=1)
        vw = lax.dynamic_slice_in_dim(vp, start, span, axis=1)
        kpos = start + k_off
        valid = ((jnp.abs(k_off[None, :] - q_off[:, None]) <= C_WINDOW)
                 & (kpos >= 0)[None, :] & (kpos < n_lat)[None, :])
        s_w = jnp.where(valid, gqa_logits(q, kw), NEG_INF)
        p = sink_softmax(jnp.concatenate([gqa_logits(q, kc), s_w], axis=-1))
        return gqa_values(p[..., :n_ctx], vc) + gqa_values(p[..., n_ctx:], vw)

    out_x = lax.map(block, (starts, qb)).swapaxes(0, 1).reshape(bsz, n_lat, C_Q_W)
    out_c = None
    if with_ctx_out:
        qc = q_c.reshape(bsz, n_ctx, C_KV_HEADS, rep, C_HEAD_DIM)
        out_c = gqa_values(sink_softmax(gqa_logits(qc, kc)), vc).reshape(bsz, n_ctx, C_Q_W)
    return out_c, out_x


def depthwise_conv_silu(xbc, w, b):
    ch = xbc.shape[-1]
    y = lax.conv_general_dilated(xbc, w[:, None, :].astype(xbc.dtype), window_strides=(1,),
                                 padding=((M_CONV // 2, M_CONV // 2),),
                                 dimension_numbers=('NWC', 'WIO', 'NWC'), feature_group_count=ch)
    return jax.nn.silu(y + b)


def ssd_scan(x, dt, a, bm, cm, h0, with_y):
    bsz, n = x.shape[:2]
    nc = n // M_CHUNK
    rep = M_HEADS // M_GROUPS
    xc = x.astype(jnp.float32).reshape(bsz, nc, M_CHUNK, M_GROUPS, rep, M_HEAD_DIM)
    dtc = dt.reshape(bsz, nc, M_CHUNK, M_GROUPS, rep)
    bc = bm.astype(jnp.float32).reshape(bsz, nc, M_CHUNK, M_GROUPS, M_STATE)
    cc = cm.astype(jnp.float32).reshape(bsz, nc, M_CHUNK, M_GROUPS, M_STATE)
    acs = jnp.cumsum(dtc * a.reshape(M_GROUPS, rep), axis=2)
    decay_end = jnp.exp(acs[:, :, -1:] - acs)
    states = jnp.einsum('bcjgn,bcjgr,bcjgrp->bcgrpn', bc, decay_end * dtc, xc)
    chunk_decay = jnp.exp(acs[:, :, -1])

    def step(h, inp):
        st, dec = inp
        return dec[..., None, None] * h + st, h

    h_final, h_start = lax.scan(step, h0, (jnp.moveaxis(states, 1, 0), jnp.moveaxis(chunk_decay, 1, 0)))
    if not with_y:
        return None, h_final
    h_start = jnp.moveaxis(h_start, 0, 1)
    lower = jnp.tril(jnp.ones((M_CHUNK, M_CHUNK), dtype=bool))
    seg = acs[:, :, :, None] - acs[:, :, None, :]
    lmat = jnp.exp(jnp.where(lower[:, :, None, None], seg, -jnp.inf))
    cb = jnp.einsum('bcign,bcjgn->bcijg', cc, bc)
    w = cb[..., None] * lmat * dtc[:, :, None]
    y = jnp.einsum('bcijgr,bcjgrp->bcigrp', w, xc)
    y = y + jnp.einsum('bcign,bcgrpn->bcigrp', cc, h_start) * jnp.exp(acs)[..., None]
    return y.reshape(bsz, n, M_HEADS, M_HEAD_DIM), h_final


def mixer_d(z_c, xbc_c, dt_c, z_x, xbc_x, dt_x, conv_w, conv_b, dt_bias, a_log, d_skip, norm_g, with_ctx_out):
    a = -jnp.exp(a_log.astype(jnp.float32))

    def prep(xbc, dt_raw):
        bsz, n = xbc.shape[:2]
        xs, bm, cm = split_cols(depthwise_conv_silu(xbc, conv_w, conv_b),
                                (M_INNER, M_GROUPS * M_STATE, M_GROUPS * M_STATE))
        dt = jax.nn.softplus(dt_raw.astype(jnp.float32).reshape(bsz, n, 2, M_HEADS) + dt_bias.astype(jnp.float32))
        return (xs.reshape(bsz, n, M_HEADS, M_HEAD_DIM), bm.reshape(bsz, n, M_GROUPS, M_STATE),
                cm.reshape(bsz, n, M_GROUPS, M_STATE), dt)

    def flip(t):
        return jnp.flip(t, axis=1)

    xs_c, b_c, c_c, dtv_c = prep(xbc_c, dt_c)
    xs_x, b_x, c_x, dtv_x = prep(xbc_x, dt_x)
    bsz = xs_x.shape[0]
    h0 = jnp.zeros((bsz, M_GROUPS, M_HEADS // M_GROUPS, M_HEAD_DIM, M_STATE), jnp.float32)
    yc_f, hc_f = ssd_scan(xs_c, dtv_c[:, :, 0], a[0], b_c, c_c, h0, with_ctx_out)
    yc_b, hc_b = ssd_scan(flip(xs_c), flip(dtv_c[:, :, 1]), a[1], flip(b_c), flip(c_c), h0, with_ctx_out)
    yx_f, _ = ssd_scan(xs_x, dtv_x[:, :, 0], a[0], b_x, c_x, hc_f, True)
    yx_b, _ = ssd_scan(flip(xs_x), flip(dtv_x[:, :, 1]), a[1], flip(b_x), flip(c_x), hc_b, True)

    def readout(y_f, y_b_rev, xs, z):
        y = y_f + flip(y_b_rev) + d_skip.astype(jnp.float32)[:, None] * xs.astype(jnp.float32)
        y = y.reshape(z.shape) * jax.nn.silu(z.astype(jnp.float32))
        return rms_norm(y, norm_g).astype(z.dtype)

    out_c = readout(yc_f, yc_b, xs_c, z_c) if with_ctx_out else None
    return out_c, readout(yx_f, yx_b, xs_x, z_x)


def expert_choice_ffn(h, w_router, w_gate, w_up, w_down):
    bsz, n, _ = h.shape
    cap = CAPACITY_FACTOR * n // N_EXPERTS
    aff = jax.nn.softmax(jnp.einsum('bnd,de->bne', h, w_router, preferred_element_type=jnp.float32), axis=-1)
    gate, idx = lax.top_k(jnp.swapaxes(aff, 1, 2), cap)
    bidx = jnp.arange(bsz)[:, None, None]
    xs = h[bidx, idx]
    hid = jax.nn.silu(jnp.einsum('becd,edf->becf', xs, w_gate)) * jnp.einsum('becd,edf->becf', xs, w_up)
    ye = jnp.einsum('becf,efd->becd', hid, w_down) * gate[..., None].astype(h.dtype)
    return jnp.zeros_like(h).at[bidx, idx].add(ye)


def setup_inputs(seed: int = 0) -> dict:
    key = jax.random.key(seed)
    ks = jax.random.split(key, 35)

    def nrm(i, shape, std=1.0):
        return std * jax.random.normal(ks[i], shape, jnp.float32)

    def unif(i, shape, lo, hi):
        return jax.random.uniform(ks[i], shape, jnp.float32, lo, hi)

    d = D_MODEL
    dt_m = jnp.exp(unif(26, (N_ODD, 2, M_HEADS), math.log(1e-3), math.log(1e-1)))
    return {
        'x': nrm(0, (BATCH, SEQ, d)),
        'c': nrm(1, (BATCH, d)),
        'ctx': nrm(2, (BATCH, CTX_LEN, d)),
        'c_ctx': nrm(3, (d,)),
        'mod_w': nrm(4, (DEPTH, d, 6 * d), 0.5 * d ** -0.5),
        'mod_b': nrm(5, (DEPTH, 6 * d), 0.02),
        'norm_g': 1.0 + nrm(6, (DEPTH, 2, d), 0.02),
        'ev_w_in': nrm(7, (N_EVEN, d, EVEN_IN), d ** -0.5),
        'ev_w_out': nrm(8, (N_EVEN, EVEN_MIX, d), EVEN_MIX ** -0.5),
        'a_q_norm': 1.0 + nrm(9, (N_EVEN, A_HEAD_DIM), 0.02),
        'a_k_norm': 1.0 + nrm(10, (N_EVEN, A_HEAD_DIM), 0.02),
        's5_a_re': -0.5 + nrm(11, (N_EVEN, 2, S5_GROUPS, S5_STATE), 0.01),
        's5_a_im': math.pi * jnp.arange(S5_STATE, dtype=jnp.float32) + nrm(12, (N_EVEN, 2, S5_GROUPS, S5_STATE), 0.01),
        's5_log_dt': unif(13, (N_EVEN, 2, S5_GROUPS), math.log(1e-3), math.log(1e-1)),
        's5_b_re': nrm(14, (N_EVEN, 2, S5_GROUPS, S5_STATE, S5_GROUP), (2 * S5_GROUP) ** -0.5),
        's5_b_im': nrm(15, (N_EVEN, 2, S5_GROUPS, S5_STATE, S5_GROUP), (2 * S5_GROUP) ** -0.5),
        's5_c_re': nrm(16, (N_EVEN, 2, S5_GROUPS, S5_GROUP, S5_STATE), (2 * S5_STATE) ** -0.5),
        's5_c_im': nrm(17, (N_EVEN, 2, S5_GROUPS, S5_GROUP, S5_STATE), (2 * S5_STATE) ** -0.5),
        's5_d': nrm(18, (N_EVEN, S5_W)),
        's5_glu_w': nrm(19, (N_EVEN, S5_W, S5_W), S5_W ** -0.5),
        's5_glu_b': nrm(20, (N_EVEN, S5_W), 0.02),
        'od_w_in': nrm(21, (N_ODD, d, ODD_IN), d ** -0.5),
        'od_w_out': nrm(22, (N_ODD, ODD_MIX, d), ODD_MIX ** -0.5),
        'c_sink': nrm(23, (N_ODD, C_HEADS)),
        'm_conv_w': nrm(24, (N_ODD, M_CONV, M_XBC), M_CONV ** -0.5),
        'm_conv_b': nrm(25, (N_ODD, M_XBC), 0.02),
        'm_dt_bias': dt_m + jnp.log(-jnp.expm1(-dt_m)),
        'm_a_log': jnp.log(unif(27, (N_ODD, 2, M_HEADS), 1.0, 16.0)),
        'm_d': 1.0 + nrm(28, (N_ODD, M_HEADS), 0.1),
        'm_norm_g': 1.0 + nrm(29, (N_ODD, M_INNER), 0.02),
        'moe_router': nrm(30, (DEPTH, d, N_EXPERTS), d ** -0.5),
        'moe_w_gate': nrm(31, (DEPTH, N_EXPERTS, d, EXPERT_FF), d ** -0.5),
        'moe_w_up': nrm(32, (DEPTH, N_EXPERTS, d, EXPERT_FF), d ** -0.5),
        'moe_w_down': nrm(33, (DEPTH, N_EXPERTS, EXPERT_FF, d), EXPERT_FF ** -0.5),
        'final_norm_g': 1.0 + nrm(34, (d,), 0.02),
    }


def reference(x, c, ctx, c_ctx, mod_w, mod_b, norm_g, ev_w_in, ev_w_out, a_q_norm, a_k_norm,
              s5_a_re, s5_a_im, s5_log_dt, s5_b_re, s5_b_im, s5_c_re, s5_c_im, s5_d, s5_glu_w, s5_glu_b,
              od_w_in, od_w_out, c_sink, m_conv_w, m_conv_b, m_dt_bias, m_a_log, m_d, m_norm_g,
              moe_router, moe_w_gate, moe_w_up, moe_w_down, final_norm_g):
    silu_c = jax.nn.silu(c)
    silu_cc = jax.nn.silu(c_ctx)
    for layer in range(DEPTH):
        last = layer == DEPTH - 1
        i = layer // 2
        mx = jnp.split((silu_c @ mod_w[layer] + mod_b[layer])[:, None, :], 6, axis=-1)
        mc = jnp.split(silu_cc @ mod_w[layer] + mod_b[layer], 6, axis=-1)
        hx = rms_norm(x, norm_g[layer, 0]) * (1 + mx[1]) + mx[0]
        hc = rms_norm(ctx, norm_g[layer, 0]) * (1 + mc[1]) + mc[0]
        if layer % 2 == 0:
            w_out = ev_w_out[i]
            pc = split_cols(hc @ ev_w_in[i], EVEN_SPLITS)
            px = split_cols(hx @ ev_w_in[i], EVEN_SPLITS)
            oa_c, oa_x = mixer_a(pc[0], pc[1], pc[2], px[0], px[1], px[2], a_q_norm[i], a_k_norm[i], not last)
            ob_c, ob_x = mixer_b(pc[3], px[3], s5_a_re[i], s5_a_im[i], s5_log_dt[i], s5_b_re[i], s5_b_im[i],
                                 s5_c_re[i], s5_c_im[i], s5_d[i], s5_glu_w[i], s5_glu_b[i], not last)
        else:
            w_out = od_w_out[i]
            pc = split_cols(hc @ od_w_in[i], ODD_SPLITS)
            px = split_cols(hx @ od_w_in[i], ODD_SPLITS)
            oa_c, oa_x = mixer_c(pc[0], pc[1], pc[2], px[0], px[1], px[2], c_sink[i], not last)
            ob_c, ob_x = mixer_d(pc[3], pc[4], pc[5], px[3], px[4], px[5], m_conv_w[i], m_conv_b[i],
                                 m_dt_bias[i], m_a_log[i], m_d[i], m_norm_g[i], not last)
        x = x + mx[2] * (jnp.concatenate([oa_x, ob_x], axis=-1) @ w_out)
        hx = rms_norm(x, norm_g[layer, 1]) * (1 + mx[4]) + mx[3]
        x = x + mx[5] * expert_choice_ffn(hx, moe_router[layer], moe_w_gate[layer], moe_w_up[layer], moe_w_down[layer])
        if not last:
            ctx = ctx + mc[2] * (jnp.concatenate([oa_c, ob_c], axis=-1) @ w_out)
            hc = rms_norm(ctx, norm_g[layer, 1]) * (1 + mc[4]) + mc[3]
            ctx = ctx + mc[5] * expert_choice_ffn(hc, moe_router[layer], moe_w_gate[layer], moe_w_up[layer], moe_w_down[layer])
    return rms_norm(x, final_norm_g)
```

```python
import functools
import math

import numpy as np
import jax
import jax.numpy as jnp
from jax import lax
from jax.experimental import pallas as pl
from jax.experimental.pallas import tpu as pltpu

F32 = jnp.float32
BF16 = jnp.bfloat16

D_MODEL = 2048
DEPTH = 4
GRID_W = 64
ROPE_THETA = 10000.0
NORM_EPS = 1e-6
NEG_INF = -1e30

A_HEADS, A_KV_HEADS, A_HEAD_DIM = 8, 2, 128
A_Q_W, A_KV_W = A_HEADS * A_HEAD_DIM, A_KV_HEADS * A_HEAD_DIM
S5_W, S5_GROUP, S5_STATE = 1024, 16, 64
S5_GROUPS = S5_W // S5_GROUP
C_HEADS, C_KV_HEADS, C_HEAD_DIM, C_WINDOW = 16, 2, 64, 128
C_Q_W, C_KV_W = C_HEADS * C_HEAD_DIM, C_KV_HEADS * C_HEAD_DIM
M_INNER, M_HEAD_DIM, M_GROUPS, M_STATE, M_CONV, M_CHUNK = 1024, 64, 2, 128, 3, 128
M_HEADS = M_INNER // M_HEAD_DIM
M_XBC = M_INNER + 2 * M_GROUPS * M_STATE
N_EXPERTS, EXPERT_FF, CAPACITY_FACTOR = 16, 1024, 2

V7X_VMEM_BYTES = 64 * 1024 * 1024
V7X_LANES = 128
V7X_SUBLANES = 8

ROW_TILE = 256
S5_CHUNK = 16
S5_GROUP_BLOCK = 4


def _cparams(vmem_mib):
    return pltpu.CompilerParams(vmem_limit_bytes=int(vmem_mib * 1024 * 1024))


def _rms(x):
    return x * lax.rsqrt(jnp.mean(x * x, axis=-1, keepdims=True) + NORM_EPS)


def _silu(x):
    return x * (1.0 / (1.0 + jnp.exp(-x)))


def _dot(a, b):
    return jnp.dot(a, b, preferred_element_type=F32)


def _dot_nt(a, b):
    return lax.dot_general(a, b, (((1,), (1,)), ((), ())), preferred_element_type=F32)


def _dot_tn(a, b):
    return lax.dot_general(a, b, (((0,), (0,)), ((), ())), preferred_element_type=F32)


def _split3(a):
    a0 = a.astype(BF16)
    r = a - a0.astype(F32)
    a1 = r.astype(BF16)
    a2 = (r - a1.astype(F32)).astype(BF16)
    return a0, a1, a2


def _dot3_right(a, b_exact):
    a0, a1, a2 = _split3(a)
    return _dot(a0, b_exact) + _dot(a1, b_exact) + _dot(a2, b_exact)


def _dot3_left(b_exact, a):
    a0, a1, a2 = _split3(a)
    return _dot(b_exact, a0) + _dot(b_exact, a1) + _dot(b_exact, a2)


MOD_ROWS = 16
MOD_TN = 1024


def _mod_kernel(cs_ref, w_ref, b_ref, o_ref):
    cs = _silu(cs_ref[...]).astype(BF16)
    o_ref[0] = _dot(cs, w_ref[0].astype(BF16)) + b_ref[0]


def _modulation(c, c_ctx, mod_w, mod_b):
    bsz, d = c.shape
    depth, _, n = mod_w.shape
    cs = jnp.concatenate([c, c_ctx[None], jnp.zeros((MOD_ROWS - bsz - 1, d), F32)], axis=0)
    out = pl.pallas_call(
        _mod_kernel,
        grid=(depth, n // MOD_TN),
        in_specs=[pl.BlockSpec((MOD_ROWS, d), lambda l, j: (0, 0)),
                  pl.BlockSpec((1, d, MOD_TN), lambda l, j: (l, 0, j)),
                  pl.BlockSpec((1, 1, MOD_TN), lambda l, j: (l, 0, j))],
        out_specs=pl.BlockSpec((1, MOD_ROWS, MOD_TN), lambda l, j: (l, 0, j)),
        out_shape=jax.ShapeDtypeStruct((depth, MOD_ROWS, n), F32),
        compiler_params=_cparams(40),
        name="adaln_modulation",
    )(cs, mod_w, mod_b.reshape(depth, 1, n))
    m = out.reshape(depth, MOD_ROWS, 6, d)
    mx = m[:, :bsz]
    mc = jnp.broadcast_to(m[:, bsz][:, None], mx.shape)
    mods = jnp.stack([mc, mx], axis=2)
    return jnp.pad(mods, ((0, 0), (0, 0), (0, 0), (0, 2), (0, 0)))


def _rope_tables(n_ctx, n_lat, head_dim):
    rows = n_lat // GRID_W
    n_freq = head_dim // 4
    inv = ROPE_THETA ** (-np.arange(n_freq, dtype=np.float64) / n_freq)
    row = np.repeat(np.arange(rows, dtype=np.float64), GRID_W)
    col = np.tile(np.arange(GRID_W, dtype=np.float64), rows)
    ang = np.concatenate([row[:, None] * inv, col[:, None] * inv], axis=-1)
    cos = np.concatenate([np.cos(ang), np.cos(ang)], axis=-1)
    sin = np.concatenate([-np.sin(ang), np.sin(ang)], axis=-1)
    cos = np.concatenate([np.ones((n_ctx, head_dim)), cos], axis=0)
    sin = np.concatenate([np.zeros((n_ctx, head_dim)), sin], axis=0)
    reps = V7X_LANES // head_dim
    return (jnp.asarray(np.tile(cos, (1, reps)), F32), jnp.asarray(np.tile(sin, (1, reps)), F32))


def _rope128(y, cos, sin):
    return y * cos + pltpu.roll(y, 64, 1) * sin


def _rope64(y, cos, sin):
    lane = lax.broadcasted_iota(jnp.int32, y.shape, 1)
    partner = jnp.where((lane & 63) < 32, pltpu.roll(y, 96, 1), pltpu.roll(y, 32, 1))
    return y * cos + partner * sin


def _inproj_kernel(*refs, even, has_moe):
    it = iter(refs)
    x_ref = next(it)
    moe_ref = next(it) if has_moe else None
    pmod_ref = next(it) if has_moe else None
    mod_ref, g_ref, w_ref, cos_ref, sin_ref = next(it), next(it), next(it), next(it), next(it)
    if even:
        qn_ref, kn_ref = next(it), next(it)
    xo_ref = next(it) if has_moe else None
    outs = list(it)

    x = x_ref[0]
    if has_moe:
        x = x + pmod_ref[0, 0, 5:6, :] * moe_ref[0]
        xo_ref[0] = x
    h = _rms(x) * g_ref[...]
    h = (h * (1.0 + mod_ref[0, 0, 1:2, :]) + mod_ref[0, 0, 0:1, :]).astype(BF16)
    cos, sin = cos_ref[...], sin_ref[...]

    if even:
        q_ref, k_ref, v_ref, u_ref = outs
        q = _dot(h, w_ref[:, 0:A_Q_W])
        for hh in range(A_HEADS):
            y = _rms(q[:, hh * 128:(hh + 1) * 128]) * qn_ref[...]
            q_ref[0, :, hh * 128:(hh + 1) * 128] = _rope128(y, cos, sin).astype(BF16)
        k = _dot(h, w_ref[:, A_Q_W:A_Q_W + A_KV_W])
        for hh in range(A_KV_HEADS):
            y = _rms(k[:, hh * 128:(hh + 1) * 128]) * kn_ref[...]
            k_ref[0, :, hh * 128:(hh + 1) * 128] = _rope128(y, cos, sin).astype(BF16)
        o = A_Q_W + A_KV_W
        v_ref[0] = _dot(h, w_ref[:, o:o + A_KV_W]).astype(BF16)
        o += A_KV_W
        u_ref[0] = _dot(h, w_ref[:, o:o + S5_W])
    else:
        q_ref, k_ref, v_ref, z_ref, xbc_ref, dt_ref = outs
        q = _dot(h, w_ref[:, 0:C_Q_W])
        for j in range(C_Q_W // 128):
            y = _rope64(q[:, j * 128:(j + 1) * 128], cos, sin) * (C_HEAD_DIM ** -0.5)
            q_ref[0, :, j * 128:(j + 1) * 128] = y.astype(BF16)
        o = C_Q_W
        k = _dot(h, w_ref[:, o:o + 2 * C_KV_W])
        for j in range(2 * C_KV_W // 128):
            k_ref[0, :, j * 128:(j + 1) * 128] = _rope64(k[:, j * 128:(j + 1) * 128], cos, sin).astype(BF16)
        o += 2 * C_KV_W
        v_ref[0] = _dot(h, w_ref[:, o:o + 2 * C_KV_W]).astype(BF16)
        o += 2 * C_KV_W
        z_ref[0] = _dot(h, w_ref[:, o:o + M_INNER])
        o += M_INNER
        xbc_ref[0] = _dot(h, w_ref[:, o:o + M_XBC])
        o += M_XBC
        dt_ref[0] = _dot(h, w_ref[:, o:o + 128])


def _inproj(x, moe, pmods, mods, gain, w, cos, sin, qk_gains, *, even):
    bsz, rows, d = x.shape
    has_moe = moe is not None
    nt = rows // ROW_TILE
    n_w = w.shape[1]
    row_spec = lambda width: pl.BlockSpec((1, ROW_TILE, width), lambda b, i: (b, i, 0))
    mod_spec = pl.BlockSpec((1, 1, 8, d), lambda b, i: (b, jnp.minimum(i, 1), 0, 0))
    const2 = lambda shape: pl.BlockSpec(shape, lambda b, i: (0, 0))
    in_specs, args = [row_spec(d)], [x]
    if has_moe:
        in_specs += [row_spec(d), mod_spec]
        args += [moe, pmods]
    in_specs += [mod_spec, const2((1, d)),
                 pl.BlockSpec((d, n_w), lambda b, i: (0, 0), pipeline_mode=pl.Buffered(1)),
                 pl.BlockSpec((ROW_TILE, 128), lambda b, i: (i, 0)),
                 pl.BlockSpec((ROW_TILE, 128), lambda b, i: (i, 0))]
    args += [mods, gain.reshape(1, d), w, cos, sin]
    if even:
        in_specs += [const2((1, 128)), const2((1, 128))]
        args += [qk_gains[0].reshape(1, 128), qk_gains[1].reshape(1, 128)]
        widths = [(A_Q_W, BF16), (A_KV_W, BF16), (A_KV_W, BF16), (S5_W, F32)]
    else:
        widths = [(C_Q_W, BF16), (2 * C_KV_W, BF16), (2 * C_KV_W, BF16), (M_INNER, F32), (M_XBC, F32), (128, F32)]
    out_specs, out_shape = [], []
    if has_moe:
        out_specs.append(row_spec(d))
        out_shape.append(jax.ShapeDtypeStruct((bsz, rows, d), F32))
    for wd, dt in widths:
        out_specs.append(row_spec(wd))
        out_shape.append(jax.ShapeDtypeStruct((bsz, rows, wd), dt))
    res = pl.pallas_call(
        functools.partial(_inproj_kernel, even=even, has_moe=has_moe),
        grid=(bsz, nt), in_specs=in_specs, out_specs=out_specs, out_shape=out_shape,
        compiler_params=_cparams(56),
        name="inproj_even" if even else "inproj_odd",
    )(*args)
    if has_moe:
        return res[0], res[1:]
    return x, res


def _attn_a_kernel(q_ref, k_ref, v_ref, o_ref, *, n_ctx, n_all, rep, dh):
    def run(nk):
        k = k_ref[0, 0:nk, :]
        v = v_ref[0, 0:nk, :]
        for r in range(rep):
            q = q_ref[0, :, r * dh:(r + 1) * dh]
            s = _dot_nt(q, k)
            p = jnp.exp(s - jnp.max(s, axis=-1, keepdims=True))
            l = jnp.sum(p, axis=-1, keepdims=True)
            o = _dot(p.astype(BF16), v)
            o_ref[0, :, r * dh:(r + 1) * dh] = (o / l).astype(BF16)

    @pl.when(pl.program_id(2) == 0)
    def _():
        run(n_ctx)

    @pl.when(pl.program_id(2) > 0)
    def _():
        run(n_all)


def _attn_a(q, k, v, n_ctx):
    bsz, rows, _ = q.shape
    rep = A_HEADS // A_KV_HEADS
    dh = A_HEAD_DIM
    tq = n_ctx
    return pl.pallas_call(
        functools.partial(_attn_a_kernel, n_ctx=n_ctx, n_all=rows, rep=rep, dh=dh),
        grid=(bsz, A_KV_HEADS, rows // tq),
        in_specs=[pl.BlockSpec((1, tq, rep * dh), lambda b, g, i: (b, i, g)),
                  pl.BlockSpec((1, rows, dh), lambda b, g, i: (b, 0, g)),
                  pl.BlockSpec((1, rows, dh), lambda b, g, i: (b, 0, g))],
        out_specs=pl.BlockSpec((1, tq, rep * dh), lambda b, g, i: (b, i, g)),
        out_shape=jax.ShapeDtypeStruct((bsz, rows, A_Q_W), BF16),
        compiler_params=_cparams(48),
        name="attn_a",
    )(q, k, v)


def _s5_operators(a_re, a_im, log_dt, b_re, b_im, c_re, c_im):
    t_len = S5_CHUNK
    lam = lax.complex(a_re.astype(F32), a_im.astype(F32))
    dt = jnp.exp(log_dt.astype(F32))[..., None]
    lam_dt = lam * dt
    lam_bar = jnp.exp(lam_dt)
    b = lax.complex(b_re.astype(F32), b_im.astype(F32))
    b_bar = ((lam_bar - 1.0) / lam)[..., None] * b
    cm = lax.complex(c_re.astype(F32), c_im.astype(F32))
    tau = jnp.arange(t_len + 1, dtype=F32)
    lam_pow = jnp.exp(lam_dt[None] * tau[:, None, None, None])
    kern = jnp.real(jnp.einsum('dgop,tdgp,dgpi->tdgoi', cm, lam_pow[:t_len], b_bar, precision='highest'))
    s_idx = np.arange(t_len)[:, None]
    t_idx = np.arange(t_len)[None, :]
    d_f = np.clip(t_idx - s_idx, 0, t_len - 1)
    d_b = np.clip(s_idx - t_idx, 0, t_len - 1)
    kf = jnp.where((t_idx >= s_idx)[:, :, None, None, None], kern[d_f, 0], 0.0)
    kb = jnp.where((s_idx >= t_idx)[:, :, None, None, None], kern[d_b, 1], 0.0)
    ktoep = jnp.transpose(kf + kb, (2, 0, 4, 1, 3)).reshape(S5_GROUPS, t_len * S5_GROUP, t_len * S5_GROUP)
    pw_f = lam_pow[t_len - 1 - np.arange(t_len), 0]
    pw_b = lam_pow[np.arange(t_len), 1]
    bend = []
    for pw, bb in ((pw_f, b_bar[0]), (pw_b, b_bar[1])):
        e = pw[:, :, :, None] * bb[None]
        e = jnp.transpose(e, (1, 0, 3, 2)).reshape(S5_GROUPS, t_len * S5_GROUP, S5_STATE)
        bend += [jnp.concatenate([jnp.real(e), jnp.imag(e)], -1), jnp.concatenate([jnp.imag(e), jnp.real(e)], -1)]
    bend = jnp.concatenate(bend, axis=-1)
    pc_f = lam_pow[1 + np.arange(t_len), 0]
    pc_b = lam_pow[t_len - np.arange(t_len), 1]
    cpow = []
    for pw, cc in ((pc_f, cm[0]), (pc_b, cm[1])):
        e = cc[None] * pw[:, :, None, :]
        e = jnp.transpose(e, (1, 3, 0, 2)).reshape(S5_GROUPS, S5_STATE, t_len * S5_GROUP)
        cpow.append(jnp.concatenate([jnp.real(e), -jnp.imag(e)], axis=1))
    cpow = jnp.concatenate(cpow, axis=1)
    lt = lam_pow[t_len]
    re, im = jnp.real(lt), jnp.imag(lt)
    zero = jnp.zeros_like(re[0])
    rows = []
    for d in range(2):
        rows += [jnp.concatenate([re[d], re[d]], -1), jnp.concatenate([-im[d], im[d]], -1),
                 jnp.concatenate([im[d], -im[d]], -1)]
    rows += [jnp.concatenate([zero, zero], -1)] * 2
    dec = jnp.stack(rows, axis=1)
    return ktoep.astype(BF16), bend.astype(BF16), cpow.astype(BF16), dec.astype(F32)


def _s5_kernel(u_ref, kt_ref, bend_ref, cpow_ref, dec_ref, y_ref, s_ref, h_ref, *, n_chunks, ctx_chunks, bp):
    gb = u_ref.shape[0]
    for g in range(gb):
        s_ref[g] = _dot(u_ref[g], bend_ref[g])

    def order_b(k):
        return jnp.where(k < ctx_chunks, ctx_chunks - 1 - k, n_chunks - 1 + ctx_chunks - k)

    def step(k, carry):
        new = []
        rf = pl.multiple_of(k * bp, bp)
        rb = pl.multiple_of(order_b(k) * bp, bp)
        for g in range(gb):
            hf, hfs, hb, hbs = carry[4 * g:4 * g + 4]
            a_f, b_f, bs_f = dec_ref[g, 0:1, :], dec_ref[g, 1:2, :], dec_ref[g, 2:3, :]
            a_b, b_b, bs_b = dec_ref[g, 3:4, :], dec_ref[g, 4:5, :], dec_ref[g, 5:6, :]
            h_ref[g, pl.ds(rf, bp), 0:128] = hf
            h_ref[g, pl.ds(rb, bp), 128:256] = hb
            sf = s_ref[g, pl.ds(rf, bp), 0:256]
            sb = s_ref[g, pl.ds(rb, bp), 256:512]
            new += [a_f * hf + b_f * hfs + sf[:, 0:128], a_f * hfs + bs_f * hf + sf[:, 128:256],
                    a_b * hb + b_b * hbs + sb[:, 0:128], a_b * hbs + bs_b * hb + sb[:, 128:256]]
        return tuple(new)

    zero = jnp.zeros((bp, 128), F32)
    lax.fori_loop(0, n_chunks, step, tuple([zero] * (4 * gb)))
    for g in range(gb):
        y_ref[g] = _dot(u_ref[g], kt_ref[g]) + _dot(h_ref[g].astype(BF16), cpow_ref[g])


def _s5_core(ug, ops, n_chunks, ctx_chunks, bp):
    ktoep, bend, cpow, dec = ops
    g_all, rows, width = ug.shape
    gb = S5_GROUP_BLOCK
    blk = lambda shape: pl.BlockSpec((gb,) + shape, lambda i: (i, 0, 0))
    return pl.pallas_call(
        functools.partial(_s5_kernel, n_chunks=n_chunks, ctx_chunks=ctx_chunks, bp=bp),
        grid=(g_all // gb,),
        in_specs=[blk((rows, width)), blk((width, width)), blk((width, 512)), blk((width, width)), blk((8, 128))],
        out_specs=blk((rows, width)),
        out_shape=jax.ShapeDtypeStruct((g_all, rows, width), F32),
        scratch_shapes=[pltpu.VMEM((gb, rows, 512), F32), pltpu.VMEM((gb, rows, width), F32)],
        compiler_params=_cparams(48),
        name="s5_core",
    )(ug, ktoep, bend, cpow, dec)


def _s5_readout_kernel(y_ref, u_ref, d_ref, w_ref, b_ref, o_ref):
    y = y_ref[0] + d_ref[...] * u_ref[0]
    g = jax.nn.gelu(y)
    o_ref[0] = (g * jax.nn.sigmoid(_dot(g.astype(BF16), w_ref[...]) + b_ref[...])).astype(BF16)


def _mixer_b(u, ops, d_skip, glu_w, glu_b, n_ctx):
    bsz, rows, w = u.shape
    t_len = S5_CHUNK
    n_chunks = rows // t_len
    ug = u.reshape(bsz, n_chunks, t_len, S5_GROUPS, S5_GROUP).transpose(3, 1, 0, 2, 4)
    ug = ug.reshape(S5_GROUPS, n_chunks * bsz, t_len * S5_GROUP).astype(BF16)
    yg = _s5_core(ug, ops, n_chunks, n_ctx // t_len, bsz)
    y = yg.reshape(S5_GROUPS, n_chunks, bsz, t_len, S5_GROUP).transpose(2, 1, 3, 0, 4).reshape(bsz, rows, w)
    row_spec = pl.BlockSpec((1, ROW_TILE, w), lambda b, i: (b, i, 0))
    const2 = lambda shape: pl.BlockSpec(shape, lambda b, i: (0, 0))
    return pl.pallas_call(
        _s5_readout_kernel,
        grid=(bsz, rows // ROW_TILE),
        in_specs=[row_spec, row_spec, const2((1, w)), const2((w, w)), const2((1, w))],
        out_specs=row_spec,
        out_shape=jax.ShapeDtypeStruct((bsz, rows, w), BF16),
        compiler_params=_cparams(32),
        name="s5_readout",
    )(y, u, d_skip.reshape(1, w), glu_w.astype(BF16), glu_b.reshape(1, w))


C_TQ = 128


def _attn_c_kernel(sink_ref, q_ref, k_ref, v_ref, o_ref, *, n_ctx, n_lat, tile0):
    g = pl.program_id(1)
    ti = pl.program_id(2) + tile0
    ctx_tiles = n_ctx // C_TQ
    rep = C_HEADS // C_KV_HEADS
    span = C_TQ + 2 * C_WINDOW
    lane = lax.broadcasted_iota(jnp.int32, (C_TQ, 128), 1)
    lo = lane < C_HEAD_DIM
    row2 = lax.broadcasted_iota(jnp.int32, (2 * C_TQ, 1), 0)

    def run(is_ctx):
        kc = k_ref[0, 0:n_ctx, :]
        vc = v_ref[0, 0:n_ctx, :]
        if not is_ctx:
            t = ti - ctx_tiles
            w0 = jnp.clip(t * C_TQ - C_WINDOW, -C_WINDOW, n_lat - span)
            ws = pl.multiple_of(w0 + n_ctx, C_TQ)
            kw = k_ref[0, pl.ds(ws, span), :]
            vw = v_ref[0, pl.ds(ws, span), :]
            qpos = t * C_TQ + lax.broadcasted_iota(jnp.int32, (C_TQ, span), 0)
            kpos = w0 + lax.broadcasted_iota(jnp.int32, (C_TQ, span), 1)
            valid = (jnp.abs(qpos - kpos) <= C_WINDOW) & (kpos >= 0)
            valid2 = jnp.concatenate([valid, valid], axis=0)
        for pr in range(rep // 2):
            q2 = q_ref[0, :, pr * 128:(pr + 1) * 128]
            zero = jnp.zeros_like(q2)
            qq = jnp.concatenate([jnp.where(lo, q2, zero), jnp.where(lo, zero, q2)], axis=0)
            sk = jnp.where(row2 < C_TQ, sink_ref[g * rep + 2 * pr], sink_ref[g * rep + 2 * pr + 1])
            s_c = _dot_nt(qq, kc)
            m = jnp.maximum(jnp.max(s_c, axis=-1, keepdims=True), sk)
            if not is_ctx:
                s_w = jnp.where(valid2, _dot_nt(qq, kw), NEG_INF)
                m = jnp.maximum(m, jnp.max(s_w, axis=-1, keepdims=True))
            e_c = jnp.exp(s_c - m)
            den = jnp.sum(e_c, axis=-1, keepdims=True) + jnp.exp(sk - m)
            o = _dot(e_c.astype(BF16), vc)
            if not is_ctx:
                e_w = jnp.exp(s_w - m)
                den = den + jnp.sum(e_w, axis=-1, keepdims=True)
                o = o + _dot(e_w.astype(BF16), vw)
            o = o / den
            o_ref[0, :, pr * 128:(pr + 1) * 128] = jnp.where(lo, o[0:C_TQ], o[C_TQ:2 * C_TQ]).astype(BF16)

    if tile0 == 0:
        @pl.when(ti < ctx_tiles)
        def _():
            run(True)

        @pl.when(ti >= ctx_tiles)
        def _():
            run(False)
    else:
        run(False)


def _attn_c(q, k, v, sink, n_ctx, with_ctx):
    bsz, rows, _ = q.shape
    n_lat = rows - n_ctx
    tile0 = 0 if with_ctx else n_ctx // C_TQ
    n_tiles = rows // C_TQ - tile0
    half = C_Q_W // C_KV_HEADS
    return pl.pallas_call(
        functools.partial(_attn_c_kernel, n_ctx=n_ctx, n_lat=n_lat, tile0=tile0),
        grid=(bsz, C_KV_HEADS, n_tiles),
        in_specs=[pl.BlockSpec(memory_space=pltpu.SMEM),
                  pl.BlockSpec((1, C_TQ, half), lambda b, g, i: (b, i + tile0, g)),
                  pl.BlockSpec((1, rows, 128), lambda b, g, i: (b, 0, g)),
                  pl.BlockSpec((1, rows, 128), lambda b, g, i: (b, 0, g))],
        out_specs=pl.BlockSpec((1, C_TQ, half), lambda b, g, i: (b, i + tile0, g)),
        out_shape=jax.ShapeDtypeStruct((bsz, rows, C_Q_W), BF16),
        compiler_params=_cparams(32),
        name="attn_c",
    )(sink.astype(F32), q, k, v)


def _ssd_constants():
    q = M_CHUNK
    tri_lo = np.tril(np.ones((q, q), np.float32))
    tri_up = np.triu(np.ones((q, q), np.float32))
    e = np.zeros((2, 128, M_INNER), np.float32)
    for d in range(2):
        for hd in range(M_HEADS):
            e[d, d * M_HEADS + hd, hd * M_HEAD_DIM:(hd + 1) * M_HEAD_DIM] = 1.0
    return jnp.asarray(tri_lo, BF16), jnp.asarray(tri_up, BF16), jnp.asarray(e[0], BF16), jnp.asarray(e[1], BF16)


def _ssd_conv_kernel(x_ref, xp_ref, xn_ref, w_ref, b_ref, dt_ref, dtb_ref, xo_ref, dto_ref, *, starts, ends):
    c = pl.program_id(1)
    x = x_ref[0]
    q = x.shape[0]
    row = lax.broadcasted_iota(jnp.int32, x.shape, 0)
    is_start = functools.reduce(jnp.logical_or, [c == s for s in starts])
    is_end = functools.reduce(jnp.logical_or, [c == e for e in ends])
    prev_row = jnp.where(is_start, 0.0, xp_ref[0, 7:8, :])
    next_row = jnp.where(is_end, 0.0, xn_ref[0, 0:1, :])
    xm1 = jnp.where(row == 0, prev_row, pltpu.roll(x, 1, 0))
    xp1 = jnp.where(row == q - 1, next_row, pltpu.roll(x, q - 1, 0))
    y = xm1 * w_ref[0:1, :] + x * w_ref[1:2, :] + xp1 * w_ref[2:3, :] + b_ref[...]
    xo_ref[0] = _silu(y)
    t = dt_ref[0] + dtb_ref[...]
    dto_ref[0] = jnp.maximum(t, 0.0) + jnp.log1p(jnp.exp(-jnp.abs(t)))


def _ssd_cumsums(dtv, a_vec, tri_lo, tri_up):
    q = dtv.shape[0]
    adt = dtv * a_vec
    lane = lax.broadcasted_iota(jnp.int32, adt.shape, 1)
    fwd = lane < M_HEADS
    acs = jnp.where(fwd, _dot3_left(tri_lo, adt), _dot3_left(tri_up, adt))
    tot = jnp.where(fwd[0:1], acs[q - 1:q, :], acs[0:1, :])
    return acs, tot


def _ssd_state_kernel(xf_ref, dtf_ref, xb_ref, dtb_ref, a_ref, tlo_ref, tup_ref, ef_ref, eb_ref,
                      hf_ref, hb_ref, cf_ref, cb_ref):
    k = pl.program_id(1)

    @pl.when(k == 0)
    def _():
        cf_ref[...] = jnp.zeros_like(cf_ref)
        cb_ref[...] = jnp.zeros_like(cb_ref)

    half = M_INNER // M_GROUPS
    for x_ref, dt_ref, e_ref, h_ref, c_ref in ((xf_ref, dtf_ref, ef_ref, hf_ref, cf_ref),
                                                (xb_ref, dtb_ref, eb_ref, hb_ref, cb_ref)):
        dtv = dt_ref[0]
        acs, tot = _ssd_cumsums(dtv, a_ref[...], tlo_ref[...], tup_ref[...])
        wsc = jnp.exp(tot - acs) * dtv
        xw = (x_ref[0, :, 0:M_INNER] * _dot3_right(wsc, e_ref[...])).astype(BF16)
        cd = _dot3_right(jnp.broadcast_to(jnp.exp(tot), (8, 128)), e_ref[...])[0:1, :]
        h_ref[0, 0] = c_ref[...].astype(BF16)
        for g in range(M_GROUPS):
            bm = x_ref[0, :, M_INNER + g * M_STATE:M_INNER + (g + 1) * M_STATE].astype(BF16)
            st = _dot_tn(bm, xw[:, g * half:(g + 1) * half])
            sl = slice(g * half, (g + 1) * half)
            c_ref[:, sl] = cd[:, sl] * c_ref[:, sl] + st


def _ssd_out_kernel(x_ref, dt_ref, z_ref, hf_ref, hb_ref, a_ref, tlo_ref, tup_ref, ef_ref, eb_ref, d_ref, g_ref,
                    o_ref):
    q = M_CHUNK
    half = M_INNER // M_GROUPS
    rep = M_HEADS // M_GROUPS
    dtv = dt_ref[0]
    acs, _ = _ssd_cumsums(dtv, a_ref[...], tlo_ref[...], tup_ref[...])
    acs_t = acs.T
    dt_t = dtv.T
    ea = jnp.exp(acs)
    xs = x_ref[0, :, 0:M_INNER]
    xs_b = xs.astype(BF16)
    ii = lax.broadcasted_iota(jnp.int32, (q, q), 0)
    jj = lax.broadcasted_iota(jnp.int32, (q, q), 1)
    lane = lax.broadcasted_iota(jnp.int32, (q, 128), 1)
    lo = lane < M_HEAD_DIM
    zero_b = jnp.zeros((q, 128), BF16)
    y_parts = []
    for g in range(M_GROUPS):
        bm = x_ref[0, :, M_INNER + g * M_STATE:M_INNER + (g + 1) * M_STATE].astype(BF16)
        cm = x_ref[0, :, M_INNER + (M_GROUPS + g) * M_STATE:M_INNER + (M_GROUPS + g + 1) * M_STATE].astype(BF16)
        cb = _dot_nt(cm, bm)
        ws = []
        for r in range(rep):
            hd = g * rep + r
            seg_f = acs[:, hd:hd + 1] - acs_t[hd:hd + 1, :]
            seg_b = acs[:, M_HEADS + hd:M_HEADS + hd + 1] - acs_t[M_HEADS + hd:M_HEADS + hd + 1, :]
            l_f = jnp.exp(jnp.where(ii >= jj, seg_f, NEG_INF)) * dt_t[hd:hd + 1, :]
            l_b = jnp.exp(jnp.where(jj >= ii, seg_b, NEG_INF)) * dt_t[M_HEADS + hd:M_HEADS + hd + 1, :]
            ws.append((cb * (l_f + l_b)).astype(BF16))
        for pr in range(rep // 2):
            col = (g * rep + 2 * pr) * M_HEAD_DIM
            x2 = xs_b[:, col:col + 128]
            y_parts.append(_dot(ws[2 * pr], jnp.where(lo, x2, zero_b)) + _dot(ws[2 * pr + 1], jnp.where(lo, zero_b, x2)))
    y = jnp.concatenate(y_parts, axis=1)
    cms = [x_ref[0, :, M_INNER + (M_GROUPS + g) * M_STATE:M_INNER + (M_GROUPS + g + 1) * M_STATE].astype(BF16)
           for g in range(M_GROUPS)]
    for h_ref, e_ref in ((hf_ref, ef_ref), (hb_ref, eb_ref)):
        yi = jnp.concatenate([_dot(cms[g], h_ref[0, 0, :, g * half:(g + 1) * half]) for g in range(M_GROUPS)], axis=1)
        y = y + yi * _dot3_right(ea, e_ref[...])
    y = y + d_ref[...] * xs
    y = y * _silu(z_ref[0])
    o_ref[0] = (_rms(y) * g_ref[...]).astype(BF16)


def _mixer_d(z, xbc, dt_raw, conv_w, conv_b, dt_bias, a_log, d_skip, norm_g, n_ctx, with_ctx):
    bsz, rows, _ = xbc.shape
    q = M_CHUNK
    nch = rows // q
    cch = n_ctx // q
    tri_lo, tri_up, e_f, e_b = _ssd_constants()
    a_vec = jnp.concatenate([-jnp.exp(a_log.astype(F32)).reshape(-1), jnp.zeros((128 - 2 * M_HEADS,), F32)]).reshape(1, 128)
    dtb = jnp.concatenate([dt_bias.astype(F32).reshape(-1), jnp.zeros((128 - 2 * M_HEADS,), F32)]).reshape(1, 128)
    conv_w8 = jnp.pad(conv_w.astype(F32), ((0, 8 - M_CONV), (0, 0)))
    sub = q // 8
    last8 = rows // 8 - 1
    const2 = lambda shape: pl.BlockSpec(shape, lambda b, c: (0, 0))
    xsbc, dtv = pl.pallas_call(
        functools.partial(_ssd_conv_kernel, starts=(0, cch), ends=(cch - 1, nch - 1)),
        grid=(bsz, nch),
        in_specs=[pl.BlockSpec((1, q, M_XBC), lambda b, c: (b, c, 0)),
                  pl.BlockSpec((1, 8, M_XBC), lambda b, c: (b, jnp.maximum(c * sub - 1, 0), 0)),
                  pl.BlockSpec((1, 8, M_XBC), lambda b, c: (b, jnp.minimum((c + 1) * sub, last8), 0)),
                  const2((8, M_XBC)), const2((1, M_XBC)),
                  pl.BlockSpec((1, q, 128), lambda b, c: (b, c, 0)), const2((1, 128))],
        out_specs=[pl.BlockSpec((1, q, M_XBC), lambda b, c: (b, c, 0)),
                   pl.BlockSpec((1, q, 128), lambda b, c: (b, c, 0))],
        out_shape=[jax.ShapeDtypeStruct((bsz, rows, M_XBC), F32), jax.ShapeDtypeStruct((bsz, rows, 128), F32)],
        compiler_params=_cparams(32),
        name="ssd_conv",
    )(xbc, xbc, xbc, conv_w8, conv_b.reshape(1, M_XBC), dt_raw, dtb)

    def order_b(k):
        return jnp.where(k < cch, cch - 1 - k, nch - 1 + cch - k)

    consts = [a_vec, tri_lo, tri_up, e_f, e_b]
    const_specs = [const2((1, 128)), const2((q, q)), const2((q, q)), const2((128, M_INNER)), const2((128, M_INNER))]
    h_f, h_b = pl.pallas_call(
        _ssd_state_kernel,
        grid=(bsz, nch),
        in_specs=[pl.BlockSpec((1, q, M_XBC), lambda b, k: (b, k, 0)),
                  pl.BlockSpec((1, q, 128), lambda b, k: (b, k, 0)),
                  pl.BlockSpec((1, q, M_XBC), lambda b, k: (b, order_b(k), 0)),
                  pl.BlockSpec((1, q, 128), lambda b, k: (b, order_b(k), 0))] + const_specs,
        out_specs=[pl.BlockSpec((1, 1, M_STATE, M_INNER), lambda b, k: (b, k, 0, 0)),
                   pl.BlockSpec((1, 1, M_STATE, M_INNER), lambda b, k: (b, order_b(k), 0, 0))],
        out_shape=[jax.ShapeDtypeStruct((bsz, nch, M_STATE, M_INNER), BF16)] * 2,
        scratch_shapes=[pltpu.VMEM((M_STATE, M_INNER), F32), pltpu.VMEM((M_STATE, M_INNER), F32)],
        compiler_params=_cparams(32),
        name="ssd_state",
    )(xsbc, dtv, xsbc, dtv, *consts)

    c0 = 0 if with_ctx else cch
    d_exp = jnp.repeat(d_skip.astype(F32), M_HEAD_DIM).reshape(1, M_INNER)
    chunk = lambda width: pl.BlockSpec((1, q, width), lambda b, c: (b, c + c0, 0))
    hspec = pl.BlockSpec((1, 1, M_STATE, M_INNER), lambda b, c: (b, c + c0, 0, 0))
    return pl.pallas_call(
        _ssd_out_kernel,
        grid=(bsz, nch - c0),
        in_specs=[chunk(M_XBC), chunk(128), chunk(M_INNER), hspec, hspec] + const_specs
                 + [const2((1, M_INNER)), const2((1, M_INNER))],
        out_specs=chunk(M_INNER),
        out_shape=jax.ShapeDtypeStruct((bsz, rows, M_INNER), BF16),
        compiler_params=_cparams(32),
        name="ssd_out",
    )(xsbc, dtv, z, h_f, h_b, *consts, d_exp, norm_g.reshape(1, M_INNER))


def _outproj_kernel(x_ref, oa_ref, ob_ref, mod_ref, g_ref, w_ref, wr_ref, xo_ref, h_ref, aff_ref):
    ka = oa_ref.shape[2]
    y = _dot(oa_ref[0], w_ref[0:ka, :]) + _dot(ob_ref[0], w_ref[ka:, :])
    x = x_ref[0] + mod_ref[0, 0, 2:3, :] * y
    xo_ref[0] = x
    h = _rms(x) * g_ref[...]
    h = (h * (1.0 + mod_ref[0, 0, 4:5, :]) + mod_ref[0, 0, 3:4, :]).astype(BF16)
    h_ref[0] = h
    logits = _dot_nt(wr_ref[...], h)
    e = jnp.exp(logits - jnp.max(logits, axis=0, keepdims=True))
    aff_ref[0] = e / jnp.sum(e, axis=0, keepdims=True)


def _outproj(x, oa, ob, mods, gain, w_out, w_router, tile0):
    bsz, rows, d = x.shape
    nt = rows // ROW_TILE - tile0
    row_spec = lambda width: pl.BlockSpec((1, ROW_TILE, width), lambda b, i: (b, i + tile0, 0))
    mod_spec = pl.BlockSpec((1, 1, 8, d), lambda b, i: (b, jnp.minimum(i + tile0, 1), 0, 0))
    const2 = lambda shape: pl.BlockSpec(shape, lambda b, i: (0, 0))
    return pl.pallas_call(
        _outproj_kernel,
        grid=(bsz, nt),
        in_specs=[row_spec(d), row_spec(oa.shape[2]), row_spec(ob.shape[2]), mod_spec, const2((1, d)),
                  const2(w_out.shape), const2((N_EXPERTS, d))],
        out_specs=[row_spec(d), row_spec(d), pl.BlockSpec((1, N_EXPERTS, ROW_TILE), lambda b, i: (b, 0, i + tile0))],
        out_shape=[jax.ShapeDtypeStruct((bsz, rows, d), F32), jax.ShapeDtypeStruct((bsz, rows, d), BF16),
                   jax.ShapeDtypeStruct((bsz, N_EXPERTS, rows), F32)],
        compiler_params=_cparams(48),
        name="outproj_router",
    )(x, oa, ob, mods, gain.reshape(1, d), w_out, w_router)


def _expert_kernel(xs_ref, gate_ref, wg_ref, wu_ref, wd_ref, o_ref):
    x = xs_ref[0, 0]
    hid = (_silu(_dot(x, wg_ref[0])) * _dot(x, wu_ref[0])).astype(BF16)
    o_ref[0, 0] = _dot(hid, wd_ref[0]) * gate_ref[0, 0]


def _experts(xs, gate, w_gate, w_up, w_down):
    bsz, n_e, r, d = xs.shape
    ff = w_gate.shape[2]
    slot = lambda width: pl.BlockSpec((1, 1, r, width), lambda e, b: (b, e, 0, 0))
    return pl.pallas_call(
        _expert_kernel,
        grid=(n_e, bsz),
        in_specs=[slot(d), slot(1),
                  pl.BlockSpec((1, d, ff), lambda e, b: (e, 0, 0)),
                  pl.BlockSpec((1, d, ff), lambda e, b: (e, 0, 0)),
                  pl.BlockSpec((1, ff, d), lambda e, b: (e, 0, 0))],
        out_specs=slot(d),
        out_shape=jax.ShapeDtypeStruct((bsz, n_e, r, d), F32),
        compiler_params=_cparams(48),
        name="expert_ffn",
    )(xs, gate, w_gate, w_up, w_down)


def _moe(h, aff_t, w_gate, w_up, w_down, n_ctx, with_ctx):
    bsz, rows, d = h.shape
    n_lat = rows - n_ctx
    gate_x, idx_x = lax.top_k(aff_t[:, :, n_ctx:], CAPACITY_FACTOR * n_lat // N_EXPERTS)
    idx, gate = idx_x + n_ctx, gate_x
    if with_ctx:
        gate_c, idx_c = lax.top_k(aff_t[:, :, :n_ctx], CAPACITY_FACTOR * n_ctx // N_EXPERTS)
        idx = jnp.concatenate([idx_c, idx], axis=-1)
        gate = jnp.concatenate([gate_c, gate], axis=-1)
    bidx = jnp.arange(bsz)[:, None, None]
    xs = h[bidx, idx]
    ye = _experts(xs, gate[..., None], w_gate, w_up, w_down)
    return jnp.zeros((bsz, rows, d), F32).at[bidx, idx].add(ye)


def _final_kernel(x_ref, moe_ref, mod_ref, g_ref, o_ref):
    x = x_ref[0] + mod_ref[0, 0, 5:6, :] * moe_ref[0]
    o_ref[0] = _rms(x) * g_ref[...]


def _final(x, moe, mods, gain, n_ctx):
    bsz, rows, d = x.shape
    t0 = n_ctx // ROW_TILE
    spec = pl.BlockSpec((1, ROW_TILE, d), lambda b, i: (b, i + t0, 0))
    return pl.pallas_call(
        _final_kernel,
        grid=(bsz, rows // ROW_TILE - t0),
        in_specs=[spec, spec, pl.BlockSpec((1, 1, 8, d), lambda b, i: (b, 1, 0, 0)),
                  pl.BlockSpec((1, d), lambda b, i: (0, 0))],
        out_specs=pl.BlockSpec((1, ROW_TILE, d), lambda b, i: (b, i, 0)),
        out_shape=jax.ShapeDtypeStruct((bsz, rows - n_ctx, d), F32),
        compiler_params=_cparams(32),
        name="final_norm",
    )(x, moe, mods, gain.reshape(1, d))


def _odd_w_in(w):
    o = 0
    q = w[:, o:o + C_Q_W]
    o += C_Q_W
    k = w[:, o:o + C_KV_W]
    o += C_KV_W
    v = w[:, o:o + C_KV_W]
    o += C_KV_W
    z = w[:, o:o + M_INNER]
    o += M_INNER
    xbc = w[:, o:o + M_XBC]
    o += M_XBC
    dt = w[:, o:o + 2 * M_HEADS]
    dup = lambda t: jnp.concatenate([t[:, 0:64], t[:, 0:64], t[:, 64:128], t[:, 64:128]], axis=1)
    dtp = jnp.pad(dt, ((0, 0), (0, 128 - 2 * M_HEADS)))
    return jnp.concatenate([q, dup(k), dup(v), z, xbc, dtp], axis=1)


def kernel(x, c, ctx, c_ctx, mod_w, mod_b, norm_g, ev_w_in, ev_w_out, a_q_norm, a_k_norm, s5_a_re, s5_a_im, s5_log_dt, s5_b_re, s5_b_im, s5_c_re, s5_c_im, s5_d, s5_glu_w, s5_glu_b, od_w_in, od_w_out, c_sink, m_conv_w, m_conv_b, m_dt_bias, m_a_log, m_d, m_norm_g, moe_router, moe_w_gate, moe_w_up, moe_w_down, final_norm_g):
    assert ctx.shape[1] == ROW_TILE and x.shape[1] % ROW_TILE == 0 and x.shape[0] == V7X_SUBLANES
    return _forward(x, c, ctx, c_ctx, mod_w, mod_b, norm_g, ev_w_in, ev_w_out, a_q_norm, a_k_norm, s5_a_re, s5_a_im, s5_log_dt, s5_b_re, s5_b_im, s5_c_re, s5_c_im, s5_d, s5_glu_w, s5_glu_b, od_w_in, od_w_out, c_sink, m_conv_w, m_conv_b, m_dt_bias, m_a_log, m_d, m_norm_g, moe_router, moe_w_gate, moe_w_up, moe_w_down, final_norm_g)


def _forward(x, c, ctx, c_ctx, mod_w, mod_b, norm_g, ev_w_in, ev_w_out, a_q_norm, a_k_norm, s5_a_re, s5_a_im, s5_log_dt, s5_b_re, s5_b_im, s5_c_re, s5_c_im, s5_d, s5_glu_w, s5_glu_b, od_w_in, od_w_out, c_sink, m_conv_w, m_conv_b, m_dt_bias, m_a_log, m_d, m_norm_g, moe_router, moe_w_gate, moe_w_up, moe_w_down, final_norm_g):
    bsz, n_lat, d = x.shape
    n_ctx = ctx.shape[1]
    mods = _modulation(c, c_ctx, mod_w, mod_b)
    stream = jnp.concatenate([ctx, x], axis=1)
    cos_a, sin_a = _rope_tables(n_ctx, n_lat, A_HEAD_DIM)
    cos_c, sin_c = _rope_tables(n_ctx, n_lat, C_HEAD_DIM)
    moe = None
    for layer in range(DEPTH):
        last = layer == DEPTH - 1
        i = layer // 2
        pmods = mods[layer - 1] if layer else None
        if layer % 2 == 0:
            gains = (a_q_norm[i] * (A_HEAD_DIM ** -0.5), a_k_norm[i])
            stream, (q, k, v, u) = _inproj(stream, moe, pmods, mods[layer], norm_g[layer, 0], ev_w_in[i].astype(BF16),
                                           cos_a, sin_a, gains, even=True)
            oa = _attn_a(q, k, v, n_ctx)
            ops = _s5_operators(s5_a_re[i], s5_a_im[i], s5_log_dt[i], s5_b_re[i], s5_b_im[i], s5_c_re[i], s5_c_im[i])
            ob = _mixer_b(u, ops, s5_d[i], s5_glu_w[i], s5_glu_b[i], n_ctx)
            w_out = ev_w_out[i]
        else:
            stream, (q, k, v, z, xbc, dt_raw) = _inproj(stream, moe, pmods, mods[layer], norm_g[layer, 0],
                                                        _odd_w_in(od_w_in[i]).astype(BF16), cos_c, sin_c, None, even=False)
            oa = _attn_c(q, k, v, c_sink[i], n_ctx, not last)
            ob = _mixer_d(z, xbc, dt_raw, m_conv_w[i], m_conv_b[i], m_dt_bias[i], m_a_log[i], m_d[i], m_norm_g[i],
                          n_ctx, not last)
            w_out = od_w_out[i]
        tile0 = n_ctx // ROW_TILE if last else 0
        stream, h2, aff_t = _outproj(stream, oa, ob, mods[layer], norm_g[layer, 1], w_out.astype(BF16),
                                     moe_router[layer].T.astype(BF16), tile0)
        moe = _moe(h2, aff_t, moe_w_gate[layer].astype(BF16), moe_w_up[layer].astype(BF16),
                   moe_w_down[layer].astype(BF16), n_ctx, not last)
    return _final(stream, moe, mods[DEPTH - 1], final_norm_g, n_ctx)
```
